```python
import math
import jax, jax.numpy as jnp
from jax import lax
import numpy as np

D_MODEL = 2048
BATCH = 2
SEQ = 16384
DEPTH = 2

MEM_LEN = 256
EPS = 1e-6
NEG_BIG = -1e30

MOBA_HEADS = 8
MOBA_HEAD_DIM = 64
MOBA_WIDTH = MOBA_HEADS * MOBA_HEAD_DIM
MOBA_BLOCK = 256
MOBA_TOPK = 3
MOBA_QCHUNK = 64

SSM_WIDTH = D_MODEL // 2
SSM_HEAD_DIM = 64
SSM_HEADS = SSM_WIDTH // SSM_HEAD_DIM
SSM_GROUPS = 4
SSM_STATE = 128
SSM_CONV = 4
SSM_CHUNK = 128
SSM_CONV_DIM = SSM_WIDTH + 2 * SSM_GROUPS * SSM_STATE

MLSTM_HEADS = 4
MLSTM_HEAD_DIM = 128
MLSTM_WIDTH = MLSTM_HEADS * MLSTM_HEAD_DIM
MLSTM_CONV = 4
MLSTM_CHUNK = 128

MIX_WIDTH = MOBA_WIDTH + SSM_WIDTH + MLSTM_WIDTH

XATTN_HEADS = 4
XATTN_HEAD_DIM = 128
XATTN_WIDTH = XATTN_HEADS * XATTN_HEAD_DIM

PEER_HEADS = 8
PEER_NKEYS = 128
PEER_EXPERTS = PEER_NKEYS * PEER_NKEYS
PEER_KEY_DIM = 128
PEER_HALF = PEER_KEY_DIM // 2
PEER_TOPK = 16
PEER_TOKEN_BLOCK = 128

IN_SIZES = [MOBA_WIDTH, MOBA_WIDTH, MOBA_WIDTH,
            SSM_WIDTH, SSM_CONV_DIM, SSM_HEADS,
            MLSTM_WIDTH, MLSTM_WIDTH, MLSTM_WIDTH, MLSTM_HEADS, MLSTM_HEADS]
IN_WIDTH = sum(IN_SIZES)
IN_SPLITS = [sum(IN_SIZES[:i + 1]) for i in range(len(IN_SIZES) - 1)]

kernel_name = "hybrid_moba_ssd_mlstm_peer_trunk"


def rms_norm(x, g):
    xf = x.astype(jnp.float32)
    y = xf * lax.rsqrt(jnp.mean(xf * xf, axis=-1, keepdims=True) + EPS)
    return (y * g.astype(jnp.float32)).astype(x.dtype)


def causal_dwconv(x, w, b):
    K, C = w.shape
    y = lax.conv_general_dilated(
        x, w[:, None, :].astype(x.dtype), window_strides=(1,), padding=[(K - 1, 0)],
        dimension_numbers=('NWC', 'WIO', 'NWC'), feature_group_count=C)
    return y + b.astype(x.dtype)


def alibi_slopes(n_heads):
    return jnp.asarray(2.0 ** (-8.0 * np.arange(1, n_heads + 1) / n_heads), dtype=jnp.float32)


def moba_attention(q, k, v):
    B, S, H, hd = q.shape
    L = MOBA_BLOCK
    QC = MOBA_QCHUNK
    S_pad = -(-S // L) * L
    pad = S_pad - S
    if pad:
        cfg = ((0, 0), (0, pad), (0, 0), (0, 0))
        q, k, v = jnp.pad(q, cfg), jnp.pad(k, cfg), jnp.pad(v, cfg)
    nb = S_pad // L
    n_sel = min(MOBA_TOPK, nb)
    kb = k.transpose(0, 2, 1, 3).reshape(B, H, nb, L, hd)
    vb = v.transpose(0, 2, 1, 3).reshape(B, H, nb, L, hd)
    k_mean = jnp.mean(kb.astype(jnp.float32), axis=3).astype(q.dtype)
    slopes = alibi_slopes(H)
    scale = hd ** -0.5
    nq = S_pad // QC
    q_chunks = q.transpose(0, 2, 1, 3).reshape(B, H, nq, QC, hd).transpose(2, 0, 1, 3, 4)
    b_idx = jnp.arange(B)[:, None, None, None]
    h_idx = jnp.arange(H)[None, :, None, None]

    def one_chunk(args):
        c, qc = args
        start = c * QC
        blk = start // L
        t = (start + jnp.arange(QC)).astype(jnp.float32)
        gate = jnp.einsum('bhqd,bhnd->bhqn', qc, k_mean).astype(jnp.float32)
        gate = jnp.where(jnp.arange(nb) < blk, gate, -jnp.inf)
        _, sel = lax.top_k(gate, n_sel)
        valid = sel < blk
        k_sel = kb[b_idx, h_idx, sel]
        v_sel = vb[b_idx, h_idx, sel]
        pos_sel = (sel[..., None] * L + jnp.arange(L)).astype(jnp.float32)
        s_sel = jnp.einsum('bhqd,bhqjld->bhqjl', qc, k_sel).astype(jnp.float32) * scale
        s_sel = s_sel - slopes[None, :, None, None, None] * (t[None, None, :, None, None] - pos_sel)
        s_sel = jnp.where(valid[..., None], s_sel, -jnp.inf)
        k_own = lax.dynamic_index_in_dim(kb, blk, axis=2, keepdims=False)
        v_own = lax.dynamic_index_in_dim(vb, blk, axis=2, keepdims=False)
        pos_own = (blk * L + jnp.arange(L)).astype(jnp.float32)
        dist = t[:, None] - pos_own[None, :]
        s_own = jnp.einsum('bhqd,bhld->bhql', qc, k_own).astype(jnp.float32) * scale
        s_own = s_own - slopes[None, :, None, None] * dist
        s_own = jnp.where(dist >= 0, s_own, -jnp.inf)
        scores = jnp.concatenate([s_sel.reshape(B, H, QC, n_sel * L), s_own], axis=-1)
        p = jax.nn.softmax(scores, axis=-1).astype(v.dtype)
        p_sel = p[..., :n_sel * L].reshape(B, H, QC, n_sel, L)
        p_own = p[..., n_sel * L:]
        return (jnp.einsum('bhqjl,bhqjld->bhqd', p_sel, v_sel)
                + jnp.einsum('bhql,bhld->bhqd', p_own, v_own))

    out = lax.map(one_chunk, (jnp.arange(nq), q_chunks))
    out = out.transpose(1, 0, 3, 2, 4).reshape(B, S_pad, H * hd)
    return out[:, :S]


def mamba2_ssd(z, xbc, dt_raw, conv_w, conv_b, dt_bias, A_log, D_skip, norm_g):
    Bsz, S, _ = z.shape
    G, N, P, H = SSM_GROUPS, SSM_STATE, SSM_HEAD_DIM, SSM_HEADS
    R = H // G
    L = SSM_CHUNK
    nc = S // L
    xbc = jax.nn.silu(causal_dwconv(xbc, conv_w, conv_b)).astype(jnp.float32)
    xs, Bm, Cm = jnp.split(xbc, [SSM_WIDTH, SSM_WIDTH + G * N], axis=-1)
    dt = jax.nn.softplus(dt_raw.astype(jnp.float32) + dt_bias.astype(jnp.float32))
    A = -jnp.exp(A_log.astype(jnp.float32))
    xh = xs.reshape(Bsz, nc, L, G, R, P)
    Xdt = xh * dt.reshape(Bsz, nc, L, G, R)[..., None]
    a = (dt * A).reshape(Bsz, nc, L, G, R).transpose(0, 1, 3, 4, 2)
    cs = jnp.cumsum(a, axis=-1)
    Bm = Bm.reshape(Bsz, nc, L, G, N)
    Cm = Cm.reshape(Bsz, nc, L, G, N)
    causal = jnp.tril(jnp.ones((L, L), dtype=bool))
    decay = jnp.exp(jnp.where(causal, cs[..., :, None] - cs[..., None, :], -jnp.inf))
    CB = jnp.einsum('bclgn,bcsgn->bcgls', Cm, Bm)
    y_diag = jnp.einsum('bcgrls,bcsgrp->bclgrp', CB[:, :, :, None] * decay, Xdt)
    decay_to_end = jnp.exp(cs[..., -1:] - cs)
    chunk_states = jnp.einsum('bclgn,bcgrl,bclgrp->bcgrpn', Bm, decay_to_end, Xdt)
    chunk_decay = jnp.exp(cs[..., -1])

    def step(h, inp):
        st, dec = inp
        return h * dec[..., None, None] + st, h

    h0 = jnp.zeros((Bsz, G, R, P, N), jnp.float32)
    _, h_in = lax.scan(step, h0, (chunk_states.transpose(1, 0, 2, 3, 4, 5),
                                  chunk_decay.transpose(1, 0, 2, 3)))
    h_in = h_in.transpose(1, 0, 2, 3, 4, 5)
    y_off = jnp.einsum('bclgn,bcgrpn,bcgrl->bclgrp', Cm, h_in, jnp.exp(cs))
    y = y_diag + y_off + D_skip.astype(jnp.float32).reshape(G, R)[:, :, None] * xh
    y = y.reshape(Bsz, S, SSM_WIDTH) * jax.nn.silu(z.astype(jnp.float32))
    yg = y.reshape(Bsz, S, G, SSM_WIDTH // G)
    yg = yg * lax.rsqrt(jnp.mean(yg * yg, axis=-1, keepdims=True) + EPS)
    return (yg.reshape(Bsz, S, SSM_WIDTH) * norm_g.astype(jnp.float32)).astype(z.dtype)


def mlstm(u, v, o_pre, i_pre, f_pre, conv_w, conv_b, wq, wk, i_bias, f_bias, norm_g):
    Bsz, S, _ = u.shape
    H, Dh, L = MLSTM_HEADS, MLSTM_HEAD_DIM, MLSTM_CHUNK
    nc = S // L
    c = jax.nn.silu(causal_dwconv(u, conv_w, conv_b)).reshape(Bsz, S, H, Dh)
    q = jnp.einsum('bshd,hde->bhse', c, wq.astype(c.dtype)).astype(jnp.float32)
    k = jnp.einsum('bshd,hde->bhse', c, wk.astype(c.dtype)).astype(jnp.float32) * (Dh ** -0.5)
    vh = v.reshape(Bsz, S, H, Dh).transpose(0, 2, 1, 3).astype(jnp.float32)
    q = q.reshape(Bsz, H, nc, L, Dh)
    k = k.reshape(Bsz, H, nc, L, Dh)
    vh = vh.reshape(Bsz, H, nc, L, Dh)
    ig = (i_pre.astype(jnp.float32) + i_bias.astype(jnp.float32)).transpose(0, 2, 1).reshape(Bsz, H, nc, L)
    lf = jax.nn.log_sigmoid(f_pre.astype(jnp.float32) + f_bias.astype(jnp.float32)).transpose(0, 2, 1).reshape(Bsz, H, nc, L)
    b = jnp.cumsum(lf, axis=-1)
    causal = jnp.tril(jnp.ones((L, L), dtype=bool))
    Dm = jnp.where(causal, b[..., :, None] - b[..., None, :] + ig[..., None, :], -jnp.inf)
    m_intra = jnp.max(Dm, axis=-1)
    g_end = b[..., -1:] - b + ig
    a_loc = jnp.max(g_end, axis=-1)
    w_end = jnp.exp(g_end - a_loc[..., None])
    C_loc = jnp.einsum('bhcs,bhcsk,bhcsv->bhckv', w_end, k, vh)
    n_loc = jnp.einsum('bhcs,bhcsk->bhck', w_end, k)
    b_end = b[..., -1]

    def step(carry, inp):
        C, n, m = carry
        Cl, nl, al, be = inp
        m_new = jnp.maximum(be + m, al)
        sp = jnp.exp(be + m - m_new)
        sl = jnp.exp(al - m_new)
        C_new = sp[..., None, None] * C + sl[..., None, None] * Cl
        n_new = sp[..., None] * n + sl[..., None] * nl
        return (C_new, n_new, m_new), (C, n, m)

    init = (jnp.zeros((Bsz, H, Dh, Dh), jnp.float32), jnp.zeros((Bsz, H, Dh), jnp.float32),
            jnp.full((Bsz, H), NEG_BIG, jnp.float32))
    _, (C_in, n_in, m_in) = lax.scan(step, init, (jnp.moveaxis(C_loc, 2, 0), jnp.moveaxis(n_loc, 2, 0),
                                                  jnp.moveaxis(a_loc, 2, 0), jnp.moveaxis(b_end, 2, 0)))
    C_in = jnp.moveaxis(C_in, 0, 2)
    n_in = jnp.moveaxis(n_in, 0, 2)
    m_in = jnp.moveaxis(m_in, 0, 2)
    m_inter = b + m_in[..., None]
    m_t = jnp.maximum(m_inter, m_intra)
    S_qk = jnp.einsum('bhcld,bhcsd->bhcls', q, k) * jnp.exp(Dm - m_t[..., None])
    inter_w = jnp.exp(m_inter - m_t)
    num = (jnp.einsum('bhcls,bhcsd->bhcld', S_qk, vh)
           + inter_w[..., None] * jnp.einsum('bhcld,bhcde->bhcle', q, C_in))
    den = jnp.sum(S_qk, axis=-1) + inter_w * jnp.einsum('bhcld,bhcd->bhcl', q, n_in)
    h = num / jnp.maximum(jnp.abs(den), jnp.exp(-m_t))[..., None]
    h = h * lax.rsqrt(jnp.mean(h * h, axis=-1, keepdims=True) + EPS)
    h = h * norm_g.astype(jnp.float32).reshape(H, Dh)[None, :, None, None, :]
    h = h.reshape(Bsz, H, S, Dh).transpose(0, 2, 1, 3).reshape(Bsz, S, MLSTM_WIDTH)
    return (h * jax.nn.sigmoid(o_pre.astype(jnp.float32))).astype(u.dtype)


def memory_cross_attention(h, mem_n, w_q, w_kv, w_o):
    Bsz, S, _ = h.shape
    M = mem_n.shape[1]
    q = (h @ w_q).reshape(Bsz, S, XATTN_HEADS, XATTN_HEAD_DIM)
    kv = (mem_n @ w_kv).reshape(Bsz, M, 2, XATTN_HEADS, XATTN_HEAD_DIM)
    k, v = kv[:, :, 0], kv[:, :, 1]
    s = jnp.einsum('bshd,bmhd->bhsm', q, k).astype(jnp.float32) * (XATTN_HEAD_DIM ** -0.5)
    p = jax.nn.softmax(s, axis=-1).astype(v.dtype)
    o = jnp.einsum('bhsm,bmhd->bshd', p, v).reshape(Bsz, S, XATTN_WIDTH)
    return o @ w_o


def peer_ffn(h, w_query, sub_keys, expert_down, expert_up):
    Bsz, S, D = h.shape
    T = Bsz * S
    TB, K, NK, PH = PEER_TOKEN_BLOCK, PEER_TOPK, PEER_NKEYS, PEER_HEADS
    tokens = h.reshape(T // TB, TB, D)

    def one_block(xb):
        q = (xb @ w_query).reshape(TB, PH, 2, PEER_HALF)
        s = jnp.einsum('thpd,hpnd->thpn', q, sub_keys).astype(jnp.float32)
        top_s, top_i = lax.top_k(s, K)
        cand_s = (top_s[:, :, 0, :, None] + top_s[:, :, 1, None, :]).reshape(TB, PH, K * K)
        cand_i = (top_i[:, :, 0, :, None] * NK + top_i[:, :, 1, None, :]).reshape(TB, PH, K * K)
        best_s, best_j = lax.top_k(cand_s, K)
        e_idx = jnp.take_along_axis(cand_i, best_j, axis=-1)
        gate = jax.nn.softmax(best_s, axis=-1)
        u = expert_down[e_idx]
        act = jax.nn.gelu(jnp.einsum('thkd,td->thk', u, xb).astype(jnp.float32), approximate=False)
        vv = expert_up[e_idx]
        return jnp.einsum('thk,thkd->td', (gate * act).astype(vv.dtype), vv)

    return lax.map(one_block, tokens).reshape(Bsz, S, D)


def setup_inputs(seed: int = 0) -> dict:
    key = jax.random.key(seed)
    ks = jax.random.split(key, 32)
    f32 = jnp.float32

    def nrm(k, shape, scale):
        return jax.random.normal(k, shape, f32) * scale

    def gain(k, shape):
        return 1.0 + 0.02 * jax.random.normal(k, shape, f32)

    dt0 = jnp.exp(jax.random.uniform(ks[8], (DEPTH, SSM_HEADS), f32, math.log(1e-3), math.log(1e-1)))
    dt_bias = dt0 + jnp.log(-jnp.expm1(-dt0))
    A_log = jnp.log(jax.random.uniform(ks[9], (DEPTH, SSM_HEADS), f32, 1.0, 16.0))
    f_bias = jnp.linspace(3.0, 6.0, MLSTM_HEADS, dtype=f32)[None, :] + nrm(ks[17], (DEPTH, MLSTM_HEADS), 0.1)
    return {
        "x": nrm(ks[0], (BATCH, SEQ, D_MODEL), 1.0),
        "mem": nrm(ks[1], (BATCH, MEM_LEN, D_MODEL), 1.0),
        "w_in": nrm(ks[2], (DEPTH, D_MODEL, IN_WIDTH), D_MODEL ** -0.5),
        "w_out": nrm(ks[3], (DEPTH, MIX_WIDTH, D_MODEL), MIX_WIDTH ** -0.5),
        "mix_norm_g": gain(ks[4], (DEPTH, D_MODEL)),
        "ssm_conv_w": nrm(ks[5], (DEPTH, SSM_CONV, SSM_CONV_DIM), SSM_CONV ** -0.5),
        "ssm_conv_b": nrm(ks[6], (DEPTH, SSM_CONV_DIM), 0.02),
        "ssm_dt_bias": dt_bias,
        "ssm_A_log": A_log,
        "ssm_D": 1.0 + nrm(ks[10], (DEPTH, SSM_HEADS), 0.1),
        "ssm_norm_g": gain(ks[11], (DEPTH, SSM_WIDTH)),
        "mlstm_conv_w": nrm(ks[12], (DEPTH, MLSTM_CONV, MLSTM_WIDTH), MLSTM_CONV ** -0.5),
        "mlstm_conv_b": nrm(ks[13], (DEPTH, MLSTM_WIDTH), 0.02),
        "mlstm_wq": nrm(ks[14], (DEPTH, MLSTM_HEADS, MLSTM_HEAD_DIM, MLSTM_HEAD_DIM), MLSTM_HEAD_DIM ** -0.5),
        "mlstm_wk": nrm(ks[15], (DEPTH, MLSTM_HEADS, MLSTM_HEAD_DIM, MLSTM_HEAD_DIM), MLSTM_HEAD_DIM ** -0.5),
        "mlstm_i_bias": nrm(ks[16], (DEPTH, MLSTM_HEADS), 0.1),
        "mlstm_f_bias": f_bias,
        "mlstm_norm_g": gain(ks[18], (DEPTH, MLSTM_WIDTH)),
        "xattn_norm_g": gain(ks[19], (DEPTH, D_MODEL)),
        "mem_norm_g": gain(ks[20], (DEPTH, D_MODEL)),
        "xattn_w_q": nrm(ks[21], (DEPTH, D_MODEL, XATTN_WIDTH), D_MODEL ** -0.5),
        "xattn_w_kv": nrm(ks[22], (DEPTH, D_MODEL, 2 * XATTN_WIDTH), D_MODEL ** -0.5),
        "xattn_w_o": nrm(ks[23], (DEPTH, XATTN_WIDTH, D_MODEL), XATTN_WIDTH ** -0.5),
        "ffn_norm_g": gain(ks[24], (DEPTH, D_MODEL)),
        "peer_w_query": nrm(ks[25], (DEPTH, D_MODEL, PEER_HEADS * PEER_KEY_DIM), D_MODEL ** -0.5),
        "peer_sub_keys": nrm(ks[26], (DEPTH, PEER_HEADS, 2, PEER_NKEYS, PEER_HALF), PEER_HALF ** -0.5),
        "peer_down": nrm(ks[27], (DEPTH, PEER_EXPERTS, D_MODEL), D_MODEL ** -0.5),
        "peer_up": nrm(ks[28], (DEPTH, PEER_EXPERTS, D_MODEL), 0.5 * PEER_HEADS ** -0.5),
        "final_norm_g": gain(ks[29], (D_MODEL,)),
    }


def reference(x, mem, w_in, w_out, mix_norm_g, ssm_conv_w, ssm_conv_b, ssm_dt_bias, ssm_A_log, ssm_D,
              ssm_norm_g, mlstm_conv_w, mlstm_conv_b, mlstm_wq, mlstm_wk, mlstm_i_bias, mlstm_f_bias,
              mlstm_norm_g, xattn_norm_g, mem_norm_g, xattn_w_q, xattn_w_kv, xattn_w_o, ffn_norm_g,
              peer_w_query, peer_sub_keys, peer_down, peer_up, final_norm_g):
    Bsz, S, _ = x.shape
    for l in range(DEPTH):
        hn = rms_norm(x, mix_norm_g[l])
        proj = hn @ w_in[l]
        qa, ka, va, z, xbc, dt_raw, mu, mv, mo, mi, mf = jnp.split(proj, IN_SPLITS, axis=-1)
        hshape = (Bsz, S, MOBA_HEADS, MOBA_HEAD_DIM)
        y_a = moba_attention(qa.reshape(hshape), ka.reshape(hshape), va.reshape(hshape)).astype(hn.dtype)
        y_s = mamba2_ssd(z, xbc, dt_raw, ssm_conv_w[l], ssm_conv_b[l], ssm_dt_bias[l], ssm_A_log[l],
                         ssm_D[l], ssm_norm_g[l]).astype(hn.dtype)
        y_m = mlstm(mu, mv, mo, mi, mf, mlstm_conv_w[l], mlstm_conv_b[l], mlstm_wq[l], mlstm_wk[l],
                    mlstm_i_bias[l], mlstm_f_bias[l], mlstm_norm_g[l]).astype(hn.dtype)
        x = x + jnp.concatenate([y_a, y_s, y_m], axis=-1) @ w_out[l]
        x = x + memory_cross_attention(rms_norm(x, xattn_norm_g[l]), rms_norm(mem, mem_norm_g[l]),
                                       xattn_w_q[l], xattn_w_kv[l], xattn_w_o[l])
        x = x + peer_ffn(rms_norm(x, ffn_norm_g[l]), peer_w_query[l], peer_sub_keys[l], peer_down[l], peer_up[l])
    return rms_norm(x, final_norm_g)
```

```python
import functools
import math

import jax
import jax.numpy as jnp
from jax import lax
from jax.experimental import pallas as pl
from jax.experimental.pallas import tpu as pltpu

F32 = jnp.float32
BF16 = jnp.bfloat16

D_MODEL = 2048
MEM_LEN = 256
EPS = 1e-6
NEG_BIG = -1e30

MOBA_HEADS = 8
MOBA_HEAD_DIM = 64
MOBA_WIDTH = MOBA_HEADS * MOBA_HEAD_DIM
MOBA_BLOCK = 256
MOBA_TOPK = 3
MOBA_SLOTS = 64
MOBA_VROWS = 80

SSM_WIDTH = D_MODEL // 2
SSM_HEAD_DIM = 64
SSM_HEADS = SSM_WIDTH // SSM_HEAD_DIM
SSM_GROUPS = 4
SSM_STATE = 128
SSM_CONV = 4
SSM_CHUNK = 128
SSM_CONV_DIM = SSM_WIDTH + 2 * SSM_GROUPS * SSM_STATE

MLSTM_HEADS = 4
MLSTM_HEAD_DIM = 128
MLSTM_WIDTH = MLSTM_HEADS * MLSTM_HEAD_DIM
MLSTM_CHUNK = 128

XATTN_HEADS = 4
XATTN_HEAD_DIM = 128
XATTN_WIDTH = XATTN_HEADS * XATTN_HEAD_DIM

PEER_HEADS = 8
PEER_NKEYS = 128
PEER_EXPERTS = PEER_NKEYS * PEER_NKEYS
PEER_KEY_DIM = 128
PEER_HALF = PEER_KEY_DIM // 2
PEER_TOPK = 16

IN_SIZES = [MOBA_WIDTH, MOBA_WIDTH, MOBA_WIDTH,
            SSM_WIDTH, SSM_CONV_DIM, SSM_HEADS,
            MLSTM_WIDTH, MLSTM_WIDTH, MLSTM_WIDTH, MLSTM_HEADS, MLSTM_HEADS]
IN_SPLITS = [sum(IN_SIZES[:i + 1]) for i in range(len(IN_SIZES) - 1)]

IN_TILE = 512
IN_MOBA_TILES = 3 * MOBA_WIDTH // IN_TILE
IN_SSM_TILES = (SSM_WIDTH + SSM_CONV_DIM) // IN_TILE
IN_ML_TILES = 3 * MLSTM_WIDTH // IN_TILE
IN_SMALL_WIDTH = 256
IN_NUM_TILES = IN_MOBA_TILES + IN_SSM_TILES + IN_ML_TILES + 1

V7X_VMEM_LIMIT = 56 * 1024 * 1024


def _cparams(semantics, vmem=None):
    return pltpu.CompilerParams(dimension_semantics=semantics, vmem_limit_bytes=vmem)


def _dot(a, b):
    return jnp.dot(a, b, preferred_element_type=F32)


def _dot_nt(a, b):
    return lax.dot_general(a, b, (((1,), (1,)), ((), ())), preferred_element_type=F32)


def _rms_rows(x, g):
    return x * lax.rsqrt(jnp.mean(x * x, axis=-1, keepdims=True) + EPS) * g


def _split_bf16(a, parts):
    out = []
    r = a
    for _ in range(parts):
        p = r.astype(BF16)
        out.append(p)
        r = r - p.astype(F32)
    return out


def _dot_exact_rhs(m_bf16, a_f32, parts):
    acc = None
    for p in _split_bf16(a_f32, parts):
        t = _dot(m_bf16, p)
        acc = t if acc is None else acc + t
    return acc


def _dot_exact_lhs(a_f32, m_bf16, parts):
    acc = None
    for p in _split_bf16(a_f32, parts):
        t = _dot(p, m_bf16)
        acc = t if acc is None else acc + t
    return acc


def _silu(x):
    return x * jax.nn.sigmoid(x)


def _inproj_kernel(x_ref, g_ref, w_ref, moba_ref, ssm_ref, ml_ref, small_ref, hn_ref):
    j = pl.program_id(1)

    @pl.when(j == 0)
    def _():
        hn_ref[...] = _rms_rows(x_ref[...], g_ref[...]).astype(BF16)

    r = _dot(hn_ref[...], w_ref[...])
    s0 = IN_MOBA_TILES
    s1 = s0 + IN_SSM_TILES
    s2 = s1 + IN_ML_TILES

    @pl.when(j < s0)
    def _():
        moba_ref[...] = r.astype(BF16)

    @pl.when((j >= s0) & (j < s1))
    def _():
        ssm_ref[...] = r

    @pl.when((j >= s1) & (j < s2))
    def _():
        ml_ref[...] = r

    @pl.when(j == s2)
    def _():
        small_ref[...] = r[:, :IN_SMALL_WIDTH]


def _prep_w_in(w):
    q, k, v, z, xbc, dt, mu, mv, mo, mi, mf = jnp.split(w, IN_SPLITS, axis=1)

    def pad(a, n):
        return jnp.pad(a, ((0, 0), (0, n - a.shape[1])))

    small = jnp.concatenate([pad(dt, 128), pad(jnp.concatenate([mi, mf], axis=1), 128)], axis=1)
    return jnp.concatenate([q, k, v, z, xbc, mu, mv, mo, pad(small, IN_TILE)], axis=1).astype(BF16)


def _inproj(x2d, gain, w_prepped, tm):
    t = x2d.shape[0]
    s0 = IN_MOBA_TILES
    s1 = s0 + IN_SSM_TILES
    s2 = s1 + IN_ML_TILES
    return pl.pallas_call(
        _inproj_kernel,
        grid=(t // tm, IN_NUM_TILES),
        in_specs=[
            pl.BlockSpec((tm, D_MODEL), lambda i, j: (i, 0)),
            pl.BlockSpec((1, D_MODEL), lambda i, j: (0, 0)),
            pl.BlockSpec((D_MODEL, IN_TILE), lambda i, j: (0, j)),
        ],
        out_specs=[
            pl.BlockSpec((tm, IN_TILE), lambda i, j: (i, jnp.clip(j, 0, s0 - 1))),
            pl.BlockSpec((tm, IN_TILE), lambda i, j: (i, jnp.clip(j - s0, 0, IN_SSM_TILES - 1))),
            pl.BlockSpec((tm, IN_TILE), lambda i, j: (i, jnp.clip(j - s1, 0, IN_ML_TILES - 1))),
            pl.BlockSpec((tm, IN_SMALL_WIDTH), lambda i, j: (i, 0)),
        ],
        out_shape=[
            jax.ShapeDtypeStruct((t, 3 * MOBA_WIDTH), BF16),
            jax.ShapeDtypeStruct((t, SSM_WIDTH + SSM_CONV_DIM), F32),
            jax.ShapeDtypeStruct((t, 3 * MLSTM_WIDTH), F32),
            jax.ShapeDtypeStruct((t, IN_SMALL_WIDTH), F32),
        ],
        scratch_shapes=[pltpu.VMEM((tm, D_MODEL), BF16)],
        compiler_params=_cparams(("parallel", "arbitrary"), V7X_VMEM_LIMIT),
        name="norm_inproj",
    )(x2d, gain, w_prepped)


def _moba_kernel(q_ref, k_ref, v_ref, o_ref, kaug_ref, vt_ref, km_ref, kmm_ref, *, nb):
    L = MOBA_BLOCK
    p = pl.program_id(1)
    i = pl.program_id(2)
    lane = lax.broadcasted_iota(jnp.int32, (L, 128), 1)

    @pl.when(i == 0)
    def _setup():
        km_ref[...] = jnp.zeros_like(km_ref)
        rowf = lax.broadcasted_iota(jnp.int32, (L, 128), 0).astype(F32)
        ones8 = jnp.ones((8, L), F32)
        zeros8 = jnp.zeros((8, L), F32)

        def blk(j, carry):
            start = pl.multiple_of(j * L, L)
            kf = k_ref[pl.ds(start, L), :].astype(F32)
            km_ref[pl.ds(j, 1), :] = jnp.mean(kf, axis=0, keepdims=True)
            slot_a = lane - MOBA_HEAD_DIM
            aug_a = jnp.where(slot_a == MOBA_SLOTS - 1, rowf, jnp.where(slot_a == j, 1.0, 0.0))
            kaug_ref[0, j] = jnp.where(lane < MOBA_HEAD_DIM, kf, aug_a).astype(BF16)
            aug_b = jnp.where(lane == MOBA_SLOTS - 1, rowf, jnp.where(lane == j, 1.0, 0.0))
            kaug_ref[1, j] = jnp.where(lane >= MOBA_HEAD_DIM, kf, aug_b).astype(BF16)
            vt = v_ref[pl.ds(start, L), :].astype(F32).T
            vt_ref[0, j] = jnp.concatenate([vt[0:MOBA_HEAD_DIM], ones8, zeros8], axis=0).astype(BF16)
            vt_ref[1, j] = jnp.concatenate([vt[MOBA_HEAD_DIM:128], ones8, zeros8], axis=0).astype(BF16)
            return carry

        lax.fori_loop(0, nb, blk, 0)
        kmt = km_ref[...].T
        rr = lax.broadcasted_iota(jnp.int32, (128, 128), 0)
        km_a = jnp.where(rr < MOBA_HEAD_DIM, pltpu.roll(kmt, MOBA_HEAD_DIM, 1), 0.0)
        km_b = jnp.where(rr >= MOBA_HEAD_DIM, kmt, 0.0)
        for hh, m in enumerate((km_a, km_b)):
            hi, lo = _split_bf16(m, 2)
            kmm_ref[hh, 0] = hi
            kmm_ref[hh, 1] = lo

    q2 = q_ref[...]
    lane_f = lane.astype(F32)
    qaugs = []
    for hh in range(2):
        head = 2 * p + hh
        slope256 = lax.shift_right_logical(jnp.int32(128), head).astype(F32)
        slope = slope256 * (1.0 / 256.0)
        is_q = (lane < MOBA_HEAD_DIM) if hh == 0 else (lane >= MOBA_HEAD_DIM)
        jslot = (lane - MOBA_HEAD_DIM) if hh == 0 else lane
        qz = jnp.where(is_q, q2, jnp.zeros_like(q2))
        gate = _dot(qz, kmm_ref[hh, 0]) + _dot(qz, kmm_ref[hh, 1])
        gm = jnp.where(jnp.logical_and(jnp.logical_not(is_q), jslot < i), gate, -jnp.inf)
        sel = jnp.zeros((L, 128), F32)
        for _ in range(MOBA_TOPK):
            mx = jnp.max(gm, axis=1, keepdims=True)
            is_max = jnp.logical_and(gm == mx, mx > -jnp.inf)
            idx = jnp.min(jnp.where(is_max, lane_f, 1e9), axis=1, keepdims=True)
            pick = lane_f == idx
            sel = jnp.where(pick, 1.0, sel)
            gm = jnp.where(pick, -jnp.inf, gm)
        bias = (jslot - i).astype(F32) * slope256
        aug = jnp.where(jslot == MOBA_SLOTS - 1, slope,
                        jnp.where(sel > 0.5, bias, jnp.where(jslot == i, 0.0, NEG_BIG)))
        qaug = jnp.where(is_q, q2.astype(F32) * (MOBA_HEAD_DIM ** -0.5), aug).astype(BF16)
        qaugs.append(qaug)

    rowk = lax.broadcasted_iota(jnp.int32, (L, L), 0)
    colq = lax.broadcasted_iota(jnp.int32, (L, L), 1)
    causal = rowk <= colq

    init = []
    for hh in range(2):
        s = jnp.where(causal, _dot_nt(kaug_ref[hh, i], qaugs[hh]), NEG_BIG)
        m = jnp.max(s, axis=0, keepdims=True)
        pr = jnp.exp(s - m)
        acc = _dot(vt_ref[hh, i], pr.astype(BF16))
        init += [m, acc]

    def body(j, carry):
        out = []
        for hh in range(2):
            m, acc = carry[2 * hh], carry[2 * hh + 1]
            s = _dot_nt(kaug_ref[hh, j], qaugs[hh])
            m_new = jnp.maximum(m, jnp.max(s, axis=0, keepdims=True))
            alpha = jnp.exp(m - m_new)
            pr = jnp.exp(s - m_new)
            acc = alpha * acc + _dot(vt_ref[hh, j], pr.astype(BF16))
            out += [m_new, acc]
        return tuple(out)

    fin = lax.fori_loop(0, i, body, tuple(init))
    outs = []
    for hh in range(2):
        acc = fin[2 * hh + 1]
        outs.append(acc[0:MOBA_HEAD_DIM] / acc[MOBA_HEAD_DIM:MOBA_HEAD_DIM + 1])
    o_ref[...] = jnp.concatenate(outs, axis=0).T.astype(BF16)


def _moba(qkv, batch, seq):
    nb = seq // MOBA_BLOCK
    assert seq % MOBA_BLOCK == 0 and nb <= MOBA_SLOTS
    L = MOBA_BLOCK
    pairs = MOBA_HEADS // 2
    return pl.pallas_call(
        functools.partial(_moba_kernel, nb=nb),
        grid=(batch, pairs, nb),
        in_specs=[
            pl.BlockSpec((L, 128), lambda b, p, i: (b * nb + i, p)),
            pl.BlockSpec((seq, 128), lambda b, p, i: (b, pairs + p)),
            pl.BlockSpec((seq, 128), lambda b, p, i: (b, 2 * pairs + p)),
        ],
        out_specs=pl.BlockSpec((L, 128), lambda b, p, i: (b * nb + i, p)),
        out_shape=jax.ShapeDtypeStruct((batch * seq, MOBA_WIDTH), BF16),
        scratch_shapes=[
            pltpu.VMEM((2, nb, L, 128), BF16),
            pltpu.VMEM((2, nb, MOBA_VROWS, L), BF16),
            pltpu.VMEM((128, 128), F32),
            pltpu.VMEM((2, 2, 128, 128), BF16),
        ],
        compiler_params=_cparams(("parallel", "parallel", "arbitrary"), V7X_VMEM_LIMIT),
        name="moba_attention",
    )(qkv, qkv, qkv)


def _causal_conv(xraw, xcat_ref, cw_ref, cb_ref, n):
    xcat_ref[8:8 + n, :] = xraw
    cw = cw_ref[...]
    conv = (cb_ref[...] + cw[3:4] * xraw + cw[2:3] * xcat_ref[7:7 + n, :]
            + cw[1:2] * xcat_ref[6:6 + n, :] + cw[0:1] * xcat_ref[5:5 + n, :])
    xcat_ref[0:8, :] = xraw[n - 8:n, :]
    return conv


def _ssd_kernel(zx_ref, sm_ref, cw_ref, cb_ref, dtb_ref, alog_ref, dch_ref, ng_ref, tri_ref, exp_ref,
                o_ref, xcat_ref, hst_ref):
    L = SSM_CHUNK
    N = SSM_STATE
    GW = SSM_WIDTH // SSM_GROUPS
    c = pl.program_id(1)

    @pl.when(c == 0)
    def _():
        xcat_ref[0:8, :] = jnp.zeros((8, SSM_CONV_DIM), F32)
        hst_ref[...] = jnp.zeros_like(hst_ref)

    z = zx_ref[:, 0:SSM_WIDTH]
    xraw = zx_ref[:, SSM_WIDTH:SSM_WIDTH + SSM_CONV_DIM]
    xbc = _silu(_causal_conv(xraw, xcat_ref, cw_ref, cb_ref, L))
    xs = xbc[:, 0:SSM_WIDTH]
    bm = xbc[:, SSM_WIDTH:SSM_WIDTH + SSM_GROUPS * N]
    cm = xbc[:, SSM_WIDTH + SSM_GROUPS * N:]

    dt = jax.nn.softplus(sm_ref[...] + dtb_ref[...])
    a = dt * (-jnp.exp(alog_ref[...]))
    cs = _dot_exact_rhs(tri_ref[...], a, 3)
    ecs = jnp.exp(cs)
    dte = jnp.exp(cs[L - 1:L, :] - cs)
    ex = _dot_exact_lhs(jnp.concatenate([dt, ecs, dte], axis=0), exp_ref[...], 2)
    dt_ch, ecs_ch, dte_ch = ex[0:L], ex[L:2 * L], ex[2 * L:3 * L]

    xdt = xs * dt_ch
    xdt_b = xdt.astype(BF16)
    xw_b = (xdt * dte_ch).astype(BF16)
    bm_b = bm.astype(BF16)
    cm_b = cm.astype(BF16)
    bm_t = bm.T
    cs_t = cs.T
    rowi = lax.broadcasted_iota(jnp.int32, (L, L), 0)
    coli = lax.broadcasted_iota(jnp.int32, (L, L), 1)
    causal = rowi >= coli
    lane = lax.broadcasted_iota(jnp.int32, (L, 128), 1)

    ys = []
    for g in range(SSM_GROUPS):
        cg = cm_b[:, g * N:(g + 1) * N]
        bg = bm_b[:, g * N:(g + 1) * N]
        cb = _dot_nt(cg, bg)
        hin = hst_ref[g]
        y_g = _dot(cg, hin.astype(BF16)) * ecs_ch[:, g * GW:(g + 1) * GW]
        halves = []
        for pr in range(2):
            h0 = 4 * g + 2 * pr
            xp = xdt_b[:, h0 * SSM_HEAD_DIM:(h0 + 2) * SSM_HEAD_DIM]
            two = []
            for hh in range(2):
                h = h0 + hh
                dec = jnp.exp(jnp.where(causal, cs[:, h:h + 1] - cs_t[h:h + 1, :], NEG_BIG))
                two.append(_dot((cb * dec).astype(BF16), xp))
            halves.append(jnp.where(lane < SSM_HEAD_DIM, two[0], two[1]))
        ys.append(y_g + jnp.concatenate(halves, axis=1))
        new = _dot(bm_t[g * N:(g + 1) * N, :].astype(BF16), xw_b[:, g * GW:(g + 1) * GW])
        hst_ref[g] = hin * ecs_ch[L - 1:L, g * GW:(g + 1) * GW] + new

    y = jnp.concatenate(ys, axis=1) + dch_ref[...] * xs
    y = y * _silu(z)
    outs = []
    for g in range(SSM_GROUPS):
        yg = y[:, g * GW:(g + 1) * GW]
        outs.append(yg * lax.rsqrt(jnp.mean(yg * yg, axis=-1, keepdims=True) + EPS))
    o_ref[...] = (jnp.concatenate(outs, axis=1) * ng_ref[...]).astype(BF16)


def _pad_row(v, n):
    return jnp.pad(v.astype(F32), (0, n - v.shape[0])).reshape(1, n)


def _tri_ones(n):
    return (jnp.arange(n)[:, None] >= jnp.arange(n)[None, :]).astype(BF16)


def _ssd(ssm_in, small, conv_w, conv_b, dt_bias, a_log, d_skip, norm_g, batch, seq):
    L = SSM_CHUNK
    nc = seq // L
    expand = (jnp.arange(128)[:, None] == (jnp.arange(SSM_WIDTH) // SSM_HEAD_DIM)[None, :]).astype(BF16)
    const = lambda shape: pl.BlockSpec(shape, lambda b, c: (0,) * len(shape))
    return pl.pallas_call(
        _ssd_kernel,
        grid=(batch, nc),
        in_specs=[
            pl.BlockSpec((L, SSM_WIDTH + SSM_CONV_DIM), lambda b, c: (b * nc + c, 0)),
            pl.BlockSpec((L, 128), lambda b, c: (b * nc + c, 0)),
            const((SSM_CONV, SSM_CONV_DIM)),
            const((1, SSM_CONV_DIM)),
            const((1, 128)),
            const((1, 128)),
            const((1, SSM_WIDTH)),
            const((1, SSM_WIDTH)),
            const((L, L)),
            const((128, SSM_WIDTH)),
        ],
        out_specs=pl.BlockSpec((L, SSM_WIDTH), lambda b, c: (b * nc + c, 0)),
        out_shape=jax.ShapeDtypeStruct((batch * seq, SSM_WIDTH), BF16),
        scratch_shapes=[
            pltpu.VMEM((L + 8, SSM_CONV_DIM), F32),
            pltpu.VMEM((SSM_GROUPS, SSM_STATE, SSM_WIDTH // SSM_GROUPS), F32),
        ],
        compiler_params=_cparams(("parallel", "arbitrary"), V7X_VMEM_LIMIT),
        name="mamba2_ssd",
    )(ssm_in, small, conv_w.astype(F32), conv_b.reshape(1, -1).astype(F32), _pad_row(dt_bias, 128),
      _pad_row(a_log, 128), jnp.repeat(d_skip.astype(F32), SSM_HEAD_DIM).reshape(1, -1),
      norm_g.reshape(1, -1).astype(F32), _tri_ones(L), expand)


def _mlstm_kernel(ml_ref, sm_ref, cw_ref, cb_ref, wq_ref, wk_ref, gb_ref, ng_ref, tri_ref,
                  o_ref, ucat_ref, c_ref, n_ref, m_ref):
    L = MLSTM_CHUNK
    DH = MLSTM_HEAD_DIM
    W = MLSTM_WIDTH
    c = pl.program_id(1)

    @pl.when(c == 0)
    def _():
        ucat_ref[0:8, :] = jnp.zeros((8, W), F32)
        c_ref[...] = jnp.zeros_like(c_ref)
        n_ref[...] = jnp.zeros_like(n_ref)
        m_ref[...] = jnp.full(m_ref.shape, NEG_BIG, F32)

    u = ml_ref[:, 0:W]
    cc = _silu(_causal_conv(u, ucat_ref, cw_ref, cb_ref, L))
    gates = sm_ref[...] + gb_ref[...]
    b = _dot_exact_rhs(tri_ref[...], jax.nn.log_sigmoid(gates), 3)
    b_t = b.T
    g_t = gates.T
    rowi = lax.broadcasted_iota(jnp.int32, (L, L), 0)
    coli = lax.broadcasted_iota(jnp.int32, (L, L), 1)
    causal = rowi >= coli

    outs = []
    for h in range(MLSTM_HEADS):
        sl = slice(h * DH, (h + 1) * DH)
        ch = cc[:, sl].astype(BF16)
        q = _dot(ch, wq_ref[h])
        k = _dot(ch, wk_ref[h]) * (DH ** -0.5)
        q_b = q.astype(BF16)
        k_b = k.astype(BF16)
        v_b = ml_ref[:, W + h * DH:W + (h + 1) * DH].astype(BF16)
        fl = MLSTM_HEADS + h
        bcol = b[:, fl:fl + 1]
        brow = b_t[fl:fl + 1, :]
        igrow = g_t[h:h + 1, :]
        igcol = gates[:, h:h + 1]
        dm = jnp.where(causal, bcol - brow + igrow, -jnp.inf)
        m_intra = jnp.max(dm, axis=1, keepdims=True)
        m_in = m_ref[h][:, 0:1]
        m_inter = bcol + m_in
        m_t = jnp.maximum(m_inter, m_intra)
        s_qk = _dot_nt(q_b, k_b) * jnp.exp(dm - m_t)
        inter_w = jnp.exp(m_inter - m_t)
        c_in = c_ref[h]
        n_in = n_ref[h]
        num = _dot(s_qk.astype(BF16), v_b) + inter_w * _dot(q_b, c_in.astype(BF16))
        den = jnp.sum(s_qk, axis=1, keepdims=True) + inter_w * jnp.sum(q * n_in, axis=1, keepdims=True)
        hv = num / jnp.maximum(jnp.abs(den), jnp.exp(-m_t))
        hv = hv * lax.rsqrt(jnp.mean(hv * hv, axis=1, keepdims=True) + EPS)
        hv = hv * ng_ref[:, sl]
        outs.append(hv * jax.nn.sigmoid(ml_ref[:, 2 * W + h * DH:2 * W + (h + 1) * DH]))
        b_end = b[L - 1:L, fl:fl + 1]
        gcol = b_end - bcol + igcol
        a_loc = jnp.max(gcol, axis=0, keepdims=True)
        kw = k * jnp.exp(gcol - a_loc)
        c_loc = _dot(kw.T.astype(BF16), v_b)
        n_loc = jnp.sum(kw, axis=0, keepdims=True)
        m_new = jnp.maximum(b_end + m_in, a_loc)
        sp = jnp.exp(b_end + m_in - m_new)
        sl_ = jnp.exp(a_loc - m_new)
        c_ref[h] = sp * c_in + sl_ * c_loc
        n_ref[h] = sp * n_in + sl_ * n_loc
        m_ref[h] = jnp.broadcast_to(m_new, (1, 128))
    o_ref[...] = jnp.concatenate(outs, axis=1).astype(BF16)


def _mlstm(ml_in, small, conv_w, conv_b, wq, wk, i_bias, f_bias, norm_g, batch, seq):
    L = MLSTM_CHUNK
    nc = seq // L
    W = MLSTM_WIDTH
    gate_bias = _pad_row(jnp.concatenate([i_bias, f_bias]), 128)
    const = lambda shape: pl.BlockSpec(shape, lambda b, c: (0,) * len(shape))
    return pl.pallas_call(
        _mlstm_kernel,
        grid=(batch, nc),
        in_specs=[
            pl.BlockSpec((L, 3 * W), lambda b, c: (b * nc + c, 0)),
            pl.BlockSpec((L, 128), lambda b, c: (b * nc + c, 1)),
            const((4, W)),
            const((1, W)),
            const((MLSTM_HEADS, MLSTM_HEAD_DIM, MLSTM_HEAD_DIM)),
            const((MLSTM_HEADS, MLSTM_HEAD_DIM, MLSTM_HEAD_DIM)),
            const((1, 128)),
            const((1, W)),
            const((L, L)),
        ],
        out_specs=pl.BlockSpec((L, W), lambda b, c: (b * nc + c, 0)),
        out_shape=jax.ShapeDtypeStruct((batch * seq, W), BF16),
        scratch_shapes=[
            pltpu.VMEM((L + 8, W), F32),
            pltpu.VMEM((MLSTM_HEADS, MLSTM_HEAD_DIM, MLSTM_HEAD_DIM), F32),
            pltpu.VMEM((MLSTM_HEADS, 1, MLSTM_HEAD_DIM), F32),
            pltpu.VMEM((MLSTM_HEADS, 1, 128), F32),
        ],
        compiler_params=_cparams(("parallel", "arbitrary"), V7X_VMEM_LIMIT),
        name="mlstm",
    )(ml_in, small, conv_w.astype(F32), conv_b.reshape(1, -1).astype(F32), wq.astype(BF16), wk.astype(BF16),
      gate_bias, norm_g.reshape(1, -1).astype(F32), _tri_ones(L))


def _outproj_kernel(x_ref, ya_ref, ys_ref, ym_ref, wa_ref, ws_ref, wm_ref, o_ref):
    o_ref[...] = (x_ref[...] + _dot(ya_ref[...], wa_ref[...]) + _dot(ys_ref[...], ws_ref[...])
                  + _dot(ym_ref[...], wm_ref[...]))


def _outproj(x2d, ya, ys, ym, w_out, tm, tn):
    t = x2d.shape[0]
    w = w_out.astype(BF16)
    wa, ws, wm = w[:MOBA_WIDTH], w[MOBA_WIDTH:MOBA_WIDTH + SSM_WIDTH], w[MOBA_WIDTH + SSM_WIDTH:]
    return pl.pallas_call(
        _outproj_kernel,
        grid=(t // tm, D_MODEL // tn),
        in_specs=[
            pl.BlockSpec((tm, tn), lambda i, j: (i, j)),
            pl.BlockSpec((tm, MOBA_WIDTH), lambda i, j: (i, 0)),
            pl.BlockSpec((tm, SSM_WIDTH), lambda i, j: (i, 0)),
            pl.BlockSpec((tm, MLSTM_WIDTH), lambda i, j: (i, 0)),
            pl.BlockSpec((MOBA_WIDTH, tn), lambda i, j: (0, j)),
            pl.BlockSpec((SSM_WIDTH, tn), lambda i, j: (0, j)),
            pl.BlockSpec((MLSTM_WIDTH, tn), lambda i, j: (0, j)),
        ],
        out_specs=pl.BlockSpec((tm, tn), lambda i, j: (i, j)),
        out_shape=jax.ShapeDtypeStruct((t, D_MODEL), F32),
        compiler_params=_cparams(("parallel", "arbitrary"), V7X_VMEM_LIMIT),
        name="out_proj",
    )(x2d, ya, ys, ym, wa, ws, wm)


def _norm_matmul_kernel(x_ref, g_ref, w_ref, o_ref):
    hn = _rms_rows(x_ref[...], g_ref[...]).astype(BF16)
    o_ref[...] = _dot(hn, w_ref[...]).astype(o_ref.dtype)


def _mem_kv(mem2d, gain, w_kv, tn):
    m = mem2d.shape[0]
    n = w_kv.shape[1]
    return pl.pallas_call(
        _norm_matmul_kernel,
        grid=(n // tn,),
        in_specs=[
            pl.BlockSpec((m, D_MODEL), lambda j: (0, 0)),
            pl.BlockSpec((1, D_MODEL), lambda j: (0, 0)),
            pl.BlockSpec((D_MODEL, tn), lambda j: (0, j)),
        ],
        out_specs=pl.BlockSpec((m, tn), lambda j: (0, j)),
        out_shape=jax.ShapeDtypeStruct((m, n), BF16),
        compiler_params=_cparams(("arbitrary",), V7X_VMEM_LIMIT),
        name="mem_kv_proj",
    )(mem2d, gain, w_kv.astype(BF16))


def _xattn_kernel(x_ref, g_ref, wq_ref, kv_ref, wo_ref, o_ref):
    x = x_ref[...]
    hn = _rms_rows(x, g_ref[...]).astype(BF16)
    q = _dot(hn, wq_ref[...])
    heads = []
    for h in range(XATTN_HEADS):
        sl = slice(h * XATTN_HEAD_DIM, (h + 1) * XATTN_HEAD_DIM)
        k_h = kv_ref[:, sl]
        v_h = kv_ref[:, XATTN_WIDTH + h * XATTN_HEAD_DIM:XATTN_WIDTH + (h + 1) * XATTN_HEAD_DIM]
        s = _dot_nt(q[:, sl].astype(BF16), k_h) * (XATTN_HEAD_DIM ** -0.5)
        m = jnp.max(s, axis=1, keepdims=True)
        pr = jnp.exp(s - m)
        l = jnp.sum(pr, axis=1, keepdims=True)
        heads.append(_dot(pr.astype(BF16), v_h) / l)
    o = jnp.concatenate(heads, axis=1).astype(BF16)
    o_ref[...] = x + _dot(o, wo_ref[...])


def _xattn(x2d, gain, w_q, kv, w_o, batch, seq, tm):
    nt = seq // tm
    return pl.pallas_call(
        _xattn_kernel,
        grid=(batch, nt),
        in_specs=[
            pl.BlockSpec((tm, D_MODEL), lambda b, i: (b * nt + i, 0)),
            pl.BlockSpec((1, D_MODEL), lambda b, i: (0, 0)),
            pl.BlockSpec((D_MODEL, XATTN_WIDTH), lambda b, i: (0, 0)),
            pl.BlockSpec((MEM_LEN, 2 * XATTN_WIDTH), lambda b, i: (b, 0)),
            pl.BlockSpec((XATTN_WIDTH, D_MODEL), lambda b, i: (0, 0)),
        ],
        out_specs=pl.BlockSpec((tm, D_MODEL), lambda b, i: (b * nt + i, 0)),
        out_shape=jax.ShapeDtypeStruct((batch * seq, D_MODEL), F32),
        compiler_params=_cparams(("parallel", "parallel"), V7X_VMEM_LIMIT),
        name="memory_xattn",
    )(x2d, gain, w_q.astype(BF16), kv, w_o.astype(BF16))


def _extract_top(a, k, want_idx):
    n, tm = a.shape
    rowi = lax.broadcasted_iota(jnp.int32, (n, tm), 0).astype(F32)
    rk = lax.broadcasted_iota(jnp.int32, (k, tm), 0)
    vals = jnp.full((k, tm), -jnp.inf, F32)
    idxs = jnp.zeros((k, tm), F32)
    cur = a
    for r in range(k):
        mx = jnp.max(cur, axis=0, keepdims=True)
        idx = jnp.min(jnp.where(cur == mx, rowi, 1e9), axis=0, keepdims=True)
        vals = jnp.where(rk == r, mx, vals)
        if want_idx:
            idxs = jnp.where(rk == r, idx, idxs)
        if r + 1 < k:
            cur = jnp.where(rowi == idx, -jnp.inf, cur)
    return vals, idxs


def _peer_select_kernel(x_ref, g_ref, wqt_ref, khi_ref, klo_ref, th_ref, ea_ref, bb_ref, eb_ref):
    K = PEER_TOPK
    NK = PEER_NKEYS
    hn = _rms_rows(x_ref[...], g_ref[...]).astype(BF16)
    qy_t = _dot_nt(wqt_ref[...], hn)
    tm = qy_t.shape[1]
    rowi = lax.broadcasted_iota(jnp.int32, (NK, tm), 0).astype(F32)
    for h in range(PEER_HEADS):
        sc = []
        for p in range(2):
            e = 2 * h + p
            qs = qy_t[e * PEER_HALF:(e + 1) * PEER_HALF, :]
            q_hi, q_lo = _split_bf16(qs, 2)
            sc.append(_dot(khi_ref[e], q_hi) + _dot(khi_ref[e], q_lo) + _dot(klo_ref[e], q_hi))
        ap = sc[0] - jnp.max(sc[0], axis=0, keepdims=True)
        bp = sc[1] - jnp.max(sc[1], axis=0, keepdims=True)
        a16, ia = _extract_top(ap, K, True)
        b16, _ = _extract_top(bp, K, False)
        groups = [a16[0:1] + b16]
        for ra in range(1, 8):
            groups.append(a16[ra:ra + 1] + b16[0:8])
        groups.append(a16[8:16] + b16[0:1])
        cand = jnp.concatenate(groups, axis=0)
        tau, _ = _extract_top(cand, K, False)
        tau = tau[K - 1:K]
        chosen = cand >= tau
        zsum = jnp.sum(jnp.where(chosen, jnp.exp(cand), 0.0), axis=0, keepdims=True)
        thr = [jnp.min(jnp.where(chosen[0:16], b16, jnp.inf), axis=0, keepdims=True)]
        for ra in range(1, 8):
            lo = 16 + 8 * (ra - 1)
            thr.append(jnp.min(jnp.where(chosen[lo:lo + 8], b16[0:8], jnp.inf), axis=0, keepdims=True))
        tail = jnp.where(chosen[72:80], b16[0:1], jnp.inf)
        theta = jnp.full((NK, tm), jnp.inf, F32)
        for ra in range(K):
            t_ra = thr[ra] if ra < 8 else tail[ra - 8:ra - 7]
            theta = jnp.where(rowi == ia[ra:ra + 1], t_ra, theta)
        th_ref[h] = theta
        ea_ref[h] = jnp.exp(ap) / zsum
        bb_ref[h] = bp
        eb_ref[h] = jnp.exp(bp)


def _peer_select(x2d, gain, wq_t, k_hi, k_lo, tm):
    t = x2d.shape[0]
    sel = jax.ShapeDtypeStruct((PEER_HEADS, PEER_NKEYS, t), F32)
    sel_spec = pl.BlockSpec((PEER_HEADS, PEER_NKEYS, tm), lambda i: (0, 0, i))
    return pl.pallas_call(
        _peer_select_kernel,
        grid=(t // tm,),
        in_specs=[
            pl.BlockSpec((tm, D_MODEL), lambda i: (i, 0)),
            pl.BlockSpec((1, D_MODEL), lambda i: (0, 0)),
            pl.BlockSpec((PEER_HEADS * PEER_KEY_DIM, D_MODEL), lambda i: (0, 0)),
            pl.BlockSpec((2 * PEER_HEADS, PEER_NKEYS, PEER_HALF), lambda i: (0, 0, 0)),
            pl.BlockSpec((2 * PEER_HEADS, PEER_NKEYS, PEER_HALF), lambda i: (0, 0, 0)),
        ],
        out_specs=[sel_spec] * 4,
        out_shape=[sel] * 4,
        compiler_params=_cparams(("parallel",), V7X_VMEM_LIMIT),
        name="peer_select",
    )(x2d, gain, wq_t, k_hi, k_lo)


def _peer_expert_kernel(x_ref, g_ref, down_ref, upt_ref, th_ref, ea_ref, bb_ref, eb_ref, *rest, n_chunks, final):
    if final:
        fg_ref, o_ref, hn_ref, acc_ref = rest
    else:
        o_ref, hn_ref, acc_ref = rest
    c = pl.program_id(1)
    ec = down_ref.shape[0]
    rows_per_chunk = ec // PEER_NKEYS

    @pl.when(c == 0)
    def _():
        hn_ref[...] = _rms_rows(x_ref[...], g_ref[...]).astype(BF16)
        acc_ref[...] = jnp.zeros_like(acc_ref)

    act = _dot_nt(down_ref[...], hn_ref[...])
    gel = 0.5 * act * (1.0 + lax.erf(act * (2.0 ** -0.5)))
    blocks = []
    for ii in range(rows_per_chunk):
        irow = c * rows_per_chunk + ii
        w = None
        for h in range(PEER_HEADS):
            th_i = th_ref[h, pl.ds(irow, 1), :]
            ea_i = ea_ref[h, pl.ds(irow, 1), :]
            t = ea_i * jnp.where(bb_ref[h] >= th_i, eb_ref[h], 0.0)
            w = t if w is None else w + t
        blocks.append(w)
    wgt = jnp.concatenate(blocks, axis=0) if len(blocks) > 1 else blocks[0]
    acc_ref[...] += _dot(upt_ref[...], (wgt * gel).astype(BF16))

    @pl.when(c == n_chunks - 1)
    def _():
        y = x_ref[...] + acc_ref[...].T
        if final:
            y = _rms_rows(y, fg_ref[...])
        o_ref[...] = y


def _peer_experts(x2d, gain, down_b, up_t, sel, final_gain, tm, ec):
    t = x2d.shape[0]
    n_chunks = PEER_EXPERTS // ec
    final = final_gain is not None
    sel_spec = pl.BlockSpec((PEER_HEADS, PEER_NKEYS, tm), lambda i, c: (0, 0, i), pipeline_mode=pl.Buffered(1))
    in_specs = [
        pl.BlockSpec((tm, D_MODEL), lambda i, c: (i, 0), pipeline_mode=pl.Buffered(1)),
        pl.BlockSpec((1, D_MODEL), lambda i, c: (0, 0)),
        pl.BlockSpec((ec, D_MODEL), lambda i, c: (c, 0)),
        pl.BlockSpec((D_MODEL, ec), lambda i, c: (0, c)),
        sel_spec, sel_spec, sel_spec, sel_spec,
    ]
    args = [x2d, gain, down_b, up_t, *sel]
    if final:
        in_specs.append(pl.BlockSpec((1, D_MODEL), lambda i, c: (0, 0)))
        args.append(final_gain)
    return pl.pallas_call(
        functools.partial(_peer_expert_kernel, n_chunks=n_chunks, final=final),
        grid=(t // tm, n_chunks),
        in_specs=in_specs,
        out_specs=pl.BlockSpec((tm, D_MODEL), lambda i, c: (i, 0)),
        out_shape=jax.ShapeDtypeStruct((t, D_MODEL), F32),
        scratch_shapes=[pltpu.VMEM((tm, D_MODEL), BF16), pltpu.VMEM((D_MODEL, tm), F32)],
        compiler_params=_cparams(("parallel", "arbitrary"), V7X_VMEM_LIMIT),
        name="peer_experts",
    )(*args)


def _row(v):
    return v.reshape(1, -1).astype(F32)


def kernel(x, mem, w_in, w_out, mix_norm_g, ssm_conv_w, ssm_conv_b, ssm_dt_bias, ssm_A_log, ssm_D, ssm_norm_g, mlstm_conv_w, mlstm_conv_b, mlstm_wq, mlstm_wk, mlstm_i_bias, mlstm_f_bias, mlstm_norm_g, xattn_norm_g, mem_norm_g, xattn_w_q, xattn_w_kv, xattn_w_o, ffn_norm_g, peer_w_query, peer_sub_keys, peer_down, peer_up, final_norm_g):
    batch, seq, d = x.shape
    depth = w_in.shape[0]
    t = batch * seq
    x2d = x.reshape(t, d).astype(F32)
    mem2d = mem.reshape(batch * MEM_LEN, d).astype(F32)
    tm_proj = min(1024, t)
    tm_tok = min(512, seq)
    for l in range(depth):
        moba_in, ssm_in, ml_in, small = _inproj(x2d, _row(mix_norm_g[l]), _prep_w_in(w_in[l]), tm_proj)
        y_a = _moba(moba_in, batch, seq)
        y_s = _ssd(ssm_in, small, ssm_conv_w[l], ssm_conv_b[l], ssm_dt_bias[l], ssm_A_log[l], ssm_D[l],
                   ssm_norm_g[l], batch, seq)
        y_m = _mlstm(ml_in, small, mlstm_conv_w[l], mlstm_conv_b[l], mlstm_wq[l], mlstm_wk[l],
                     mlstm_i_bias[l], mlstm_f_bias[l], mlstm_norm_g[l], batch, seq)
        x2d = _outproj(x2d, y_a, y_s, y_m, w_out[l], tm_proj, 512)
        kv = _mem_kv(mem2d, _row(mem_norm_g[l]), xattn_w_kv[l], 512)
        x2d = _xattn(x2d, _row(xattn_norm_g[l]), xattn_w_q[l], kv, xattn_w_o[l], batch, seq, tm_tok)
        keys = peer_sub_keys[l].reshape(2 * PEER_HEADS, PEER_NKEYS, PEER_HALF).astype(F32)
        k_hi = keys.astype(BF16)
        k_lo = (keys - k_hi.astype(F32)).astype(BF16)
        sel = _peer_select(x2d, _row(ffn_norm_g[l]), peer_w_query[l].T.astype(BF16), k_hi, k_lo, min(256, t))
        x2d = _peer_experts(x2d, _row(ffn_norm_g[l]), peer_down[l].astype(BF16), peer_up[l].T.astype(BF16), sel,
                            _row(final_norm_g) if l == depth - 1 else None, tm_tok, 512)
    return x2d.reshape(batch, seq, d).astype(x.dtype)
```
